```python
import math
import jax, jax.numpy as jnp
from jax import lax
import numpy as np

D_MODEL = 1024
BATCH = 16
SEQ = 2048
DEPTH = 2

D_MIX = D_MODEL
GLA_WIDTH = D_MIX // 4
GLA_HEADS = 4
GLA_DV = GLA_WIDTH // GLA_HEADS
GLA_DK = GLA_DV // 2
GLA_KEY = GLA_HEADS * GLA_DK
GLA_GATE_RANK = 16
GLA_TAU = 16.0
GLA_CHUNK = 64
SGU_WIDTH = D_MIX // 4
SGU_GROUPS = 4
SGU_DG = SGU_WIDTH // SGU_GROUPS
SGU_CHUNK = 128
SSD_WIDTH = D_MIX // 2
SSD_HEAD_DIM = 64
SSD_HEADS = SSD_WIDTH // SSD_HEAD_DIM
SSD_GROUPS = 2
SSD_STATE = 128
SSD_CONV = 4
SSD_CHUNK = 128
SSD_CONV_DIM = SSD_WIDTH + 2 * SSD_GROUPS * SSD_STATE
D_FF = 2816
FFN_CONV = 3
IN_SPLITS = (GLA_KEY, GLA_KEY, GLA_WIDTH, GLA_WIDTH, GLA_GATE_RANK,
             SGU_WIDTH, SGU_WIDTH,
             SSD_WIDTH, SSD_CONV_DIM, SSD_HEADS)
D_IN_PROJ = sum(IN_SPLITS)
LN_EPS = 1e-5
DEEPNORM_ALPHA = (2 * DEPTH) ** 0.25
DEEPNORM_BETA = (8 * DEPTH) ** -0.25

kernel_name = 'hybrid_gla_sgu_ssd_deepnorm'

F32 = jnp.float32


def _offsets(sizes):
    out, acc = [], 0
    for s in sizes[:-1]:
        acc += s
        out.append(acc)
    return out


def layer_norm(x, g, b):
    xf = x.astype(F32)
    mu = jnp.mean(xf, axis=-1, keepdims=True)
    var = jnp.mean(jnp.square(xf - mu), axis=-1, keepdims=True)
    return ((xf - mu) * lax.rsqrt(var + LN_EPS) * g + b).astype(x.dtype)


def rms_norm(x, g):
    xf = x.astype(F32)
    ms = jnp.mean(jnp.square(xf), axis=-1, keepdims=True)
    return (xf * lax.rsqrt(ms + LN_EPS) * g).astype(x.dtype)


def causal_dwconv(x, w, b):
    k = w.shape[0]
    y = lax.conv_general_dilated(x, w[:, None, :], window_strides=(1,), padding=[(k - 1, 0)],
                                 dimension_numbers=('NWC', 'WIO', 'NWC'),
                                 feature_group_count=x.shape[-1])
    return y + b


def segsum(a):
    t = a.shape[-1]
    aa = jnp.broadcast_to(a[..., :, None], a.shape + (t,))
    aa = jnp.where(jnp.tril(jnp.ones((t, t), bool), -1), aa, 0.0)
    cs = jnp.cumsum(aa, axis=-2)
    return jnp.where(jnp.tril(jnp.ones((t, t), bool)), cs, -jnp.inf)


def gla_mixer(q, k, v, g, gate_lr, w_gate, b_gate, norm_g):
    bsz, s, _ = q.shape
    n = s // GLA_CHUNK
    log_a = jax.nn.log_sigmoid((gate_lr @ w_gate + b_gate).astype(F32)) / GLA_TAU

    def heads(t, d):
        return t.astype(F32).reshape(bsz, n, GLA_CHUNK, GLA_HEADS, d).transpose(1, 0, 3, 2, 4)

    qh = heads(q, GLA_DK) * (GLA_DK ** -0.5)
    kh = heads(k, GLA_DK)
    vh = heads(v, GLA_DV)
    ah = heads(log_a, GLA_DK)
    causal = jnp.tril(jnp.ones((GLA_CHUNK, GLA_CHUNK), bool))[:, :, None]

    def step(state, inp):
        qc, kc, vc, ac = inp
        cum = jnp.cumsum(ac, axis=-2)
        o_inter = jnp.einsum('bhik,bhkv->bhiv', qc * jnp.exp(cum), state)
        diff = cum[:, :, :, None, :] - cum[:, :, None, :, :]
        decay = jnp.exp(jnp.where(causal, diff, -jnp.inf))
        scores = jnp.einsum('bhik,bhjk,bhijk->bhij', qc, kc, decay)
        o_intra = jnp.einsum('bhij,bhjv->bhiv', scores, vc)
        last = cum[:, :, -1:, :]
        state = (jnp.exp(last[:, :, 0, :])[..., None] * state
                 + jnp.einsum('bhjk,bhjv->bhkv', kc * jnp.exp(last - cum), vc))
        return state, o_inter + o_intra

    init = jnp.zeros((bsz, GLA_HEADS, GLA_DK, GLA_DV), F32)
    _, o = lax.scan(step, init, (qh, kh, vh, ah))
    o = o.transpose(1, 0, 3, 2, 4).reshape(bsz, s, GLA_HEADS, GLA_DV)
    o = rms_norm(o, norm_g).reshape(bsz, s, GLA_WIDTH) * jax.nn.silu(g.astype(F32))
    return o.astype(q.dtype)


def sgu_mixer(u, v, norm_g, norm_b, w_s, b_s):
    bsz, s, _ = u.shape
    n = s // SGU_CHUNK
    u = jax.nn.gelu(u)
    v = layer_norm(jax.nn.gelu(v), norm_g, norm_b)
    vh = v.reshape(bsz, n, SGU_CHUNK, SGU_GROUPS, SGU_DG)
    w = w_s * jnp.tril(jnp.ones((SGU_CHUNK, SGU_CHUNK), w_s.dtype))
    mixed = jnp.einsum('gts,bnsgc->bntgc', w, vh) + b_s.T[None, None, :, :, None]
    return u * mixed.reshape(bsz, s, SGU_WIDTH)


def ssd_mixer(z, xbc, dt_raw, conv_w, conv_b, dt_bias, a_log, d_skip, norm_g):
    bsz, s, _ = z.shape
    n, l = s // SSD_CHUNK, SSD_CHUNK
    g, r, p, ns = SSD_GROUPS, SSD_HEADS // SSD_GROUPS, SSD_HEAD_DIM, SSD_STATE
    xbc = jax.nn.silu(causal_dwconv(xbc, conv_w, conv_b))
    xs, bm, cm = jnp.split(xbc, [SSD_WIDTH, SSD_WIDTH + g * ns], axis=-1)
    dt = jax.nn.softplus(dt_raw.astype(F32) + dt_bias)
    a_cont = -jnp.exp(a_log.astype(F32))
    x = xs.astype(F32).reshape(bsz, n, l, g, r, p)
    bm = bm.astype(F32).reshape(bsz, n, l, g, ns)
    cm = cm.astype(F32).reshape(bsz, n, l, g, ns)
    xdt = x * dt.reshape(bsz, n, l, g, r)[..., None]
    a = (dt * a_cont).reshape(bsz, n, l, g, r).transpose(0, 3, 4, 1, 2)
    a_cs = jnp.cumsum(a, axis=-1)
    lmat = jnp.exp(segsum(a))
    cb = jnp.einsum('bclgn,bcsgn->bgcls', cm, bm)
    y_diag = jnp.einsum('bgcls,bgrcls,bcsgrp->bclgrp', cb, lmat, xdt)
    decay_states = jnp.exp(a_cs[..., -1:] - a_cs)
    states = jnp.einsum('bclgn,bgrcl,bclgrp->bcgrpn', bm, decay_states, xdt)
    states = jnp.concatenate([jnp.zeros_like(states[:, :1]), states], axis=1)
    chunk_decay = jnp.exp(segsum(jnp.pad(a_cs[..., -1], ((0, 0), (0, 0), (0, 0), (1, 0)))))
    states = jnp.einsum('bgrzc,bcgrpn->bzgrpn', chunk_decay, states)[:, :-1]
    y_off = jnp.einsum('bclgn,bcgrpn,bgrcl->bclgrp', cm, states, jnp.exp(a_cs))
    y = y_diag + y_off + x * d_skip.astype(F32).reshape(g, r)[:, :, None]
    y = y.reshape(bsz, s, SSD_WIDTH) * jax.nn.silu(z.astype(F32))
    y = rms_norm(y.reshape(bsz, s, g, SSD_WIDTH // g), norm_g.reshape(g, SSD_WIDTH // g))
    return y.reshape(bsz, s, SSD_WIDTH).astype(z.dtype)


def conv_ffn(x, w_up, conv_w, conv_b, w_down):
    h = causal_dwconv(x @ w_up, conv_w, conv_b)
    gate, val = jnp.split(h, 2, axis=-1)
    return (jax.nn.silu(gate) * val) @ w_down


def setup_inputs(seed: int = 0) -> dict:
    key = jax.random.key(seed)
    ks = iter(jax.random.split(key, 32))
    nrm = lambda shape, scale: jax.random.normal(next(ks), shape, F32) * scale
    dt0 = jnp.exp(jax.random.uniform(next(ks), (DEPTH, SSD_HEADS), F32)
                  * (math.log(0.1) - math.log(1e-3)) + math.log(1e-3))
    return {
        'x': nrm((BATCH, SEQ, D_MODEL), 1.0),
        'ln_in_g': 1.0 + nrm((D_MODEL,), 0.02),
        'ln_in_b': nrm((D_MODEL,), 0.02),
        'w_in': nrm((DEPTH, D_MODEL, D_IN_PROJ), D_MODEL ** -0.5),
        'gla_w_gate': nrm((DEPTH, GLA_GATE_RANK, GLA_KEY), GLA_GATE_RANK ** -0.5),
        'gla_b_gate': nrm((DEPTH, GLA_KEY), 0.1),
        'gla_norm_g': 1.0 + nrm((DEPTH, GLA_DV), 0.02),
        'sgu_norm_g': 1.0 + nrm((DEPTH, SGU_WIDTH), 0.02),
        'sgu_norm_b': nrm((DEPTH, SGU_WIDTH), 0.02),
        'sgu_w': nrm((DEPTH, SGU_GROUPS, SGU_CHUNK, SGU_CHUNK), SGU_CHUNK ** -0.5),
        'sgu_b': 1.0 + nrm((DEPTH, SGU_GROUPS, SGU_CHUNK), 0.01),
        'ssd_conv_w': nrm((DEPTH, SSD_CONV, SSD_CONV_DIM), SSD_CONV ** -0.5),
        'ssd_conv_b': nrm((DEPTH, SSD_CONV_DIM), 0.02),
        'ssd_dt_bias': dt0 + jnp.log(-jnp.expm1(-dt0)),
        'ssd_a_log': jnp.log(jax.random.uniform(next(ks), (DEPTH, SSD_HEADS), F32, 1.0, 16.0)),
        'ssd_d': 1.0 + nrm((DEPTH, SSD_HEADS), 0.1),
        'ssd_norm_g': 1.0 + nrm((DEPTH, SSD_WIDTH), 0.02),
        'w_out': nrm((DEPTH, D_MIX, D_MODEL), DEEPNORM_BETA * D_MIX ** -0.5),
        'ln1_g': 1.0 + nrm((DEPTH, D_MODEL), 0.02),
        'ln1_b': nrm((DEPTH, D_MODEL), 0.02),
        'ffn_w_up': nrm((DEPTH, D_MODEL, 2 * D_FF), D_MODEL ** -0.5),
        'ffn_conv_w': nrm((DEPTH, FFN_CONV, 2 * D_FF), FFN_CONV ** -0.5),
        'ffn_conv_b': nrm((DEPTH, 2 * D_FF), 0.02),
        'ffn_w_down': nrm((DEPTH, D_FF, D_MODEL), DEEPNORM_BETA * D_FF ** -0.5),
        'ln2_g': 1.0 + nrm((DEPTH, D_MODEL), 0.02),
        'ln2_b': nrm((DEPTH, D_MODEL), 0.02),
    }


def reference(x, ln_in_g, ln_in_b, w_in, gla_w_gate, gla_b_gate, gla_norm_g,
              sgu_norm_g, sgu_norm_b, sgu_w, sgu_b,
              ssd_conv_w, ssd_conv_b, ssd_dt_bias, ssd_a_log, ssd_d, ssd_norm_g,
              w_out, ln1_g, ln1_b, ffn_w_up, ffn_conv_w, ffn_conv_b, ffn_w_down,
              ln2_g, ln2_b):
    h = layer_norm(x, ln_in_g, ln_in_b)
    offs = _offsets(IN_SPLITS)
    for i in range(DEPTH):
        proj = h @ w_in[i]
        q, k, v, g, g_lr, su, sv, z, xbc, dt = jnp.split(proj, offs, axis=-1)
        o_gla = gla_mixer(q, k, v, g, g_lr, gla_w_gate[i], gla_b_gate[i], gla_norm_g[i])
        o_sgu = sgu_mixer(su, sv, sgu_norm_g[i], sgu_norm_b[i], sgu_w[i], sgu_b[i])
        o_ssd = ssd_mixer(z, xbc, dt, ssd_conv_w[i], ssd_conv_b[i], ssd_dt_bias[i],
                          ssd_a_log[i], ssd_d[i], ssd_norm_g[i])
        mix = jnp.concatenate([o_gla, o_sgu, o_ssd], axis=-1) @ w_out[i]
        h = layer_norm(DEEPNORM_ALPHA * h + mix, ln1_g[i], ln1_b[i])
        ffn = conv_ffn(h, ffn_w_up[i], ffn_conv_w[i], ffn_conv_b[i], ffn_w_down[i])
        h = layer_norm(DEEPNORM_ALPHA * h + ffn, ln2_g[i], ln2_b[i])
    return h
```

```python
import functools
import math

import numpy as np
import jax
import jax.numpy as jnp
from jax import lax
from jax.experimental import pallas as pl
from jax.experimental.pallas import tpu as pltpu

F32 = jnp.float32
BF16 = jnp.bfloat16
HI = lax.Precision.HIGHEST

D_MODEL = 1024
DEPTH = 2
GLA_WIDTH = 256
GLA_HEADS = 4
GLA_DV = 64
GLA_DK = 32
GLA_KEY = 128
GLA_GATE_RANK = 16
GLA_TAU = 16.0
GLA_CHUNK = 64
SGU_WIDTH = 256
SGU_GROUPS = 4
SGU_DG = 64
SGU_CHUNK = 128
SSD_WIDTH = 512
SSD_HEAD_DIM = 64
SSD_HEADS = 8
SSD_GROUPS = 2
SSD_STATE = 128
SSD_CONV = 4
SSD_CHUNK = 128
SSD_CONV_DIM = 1024
D_FF = 2816
FFN_CONV = 3
LN_EPS = 1e-5
DEEPNORM_ALPHA = (2 * DEPTH) ** 0.25

LANES = 128
SUBLANES = 8
VMEM_LIMIT_BYTES = 56 * 1024 * 1024

C_Q, C_K, C_V, C_G = 0, 128, 256, 512
C_SU, C_SV, C_Z, C_XBC, C_SM = 768, 1024, 1280, 1792, 2816
N_PROJ = 2944
SM_DT = GLA_GATE_RANK
BLK = 128
FFN_CW = 256
FFN_NC = D_FF // FFN_CW


def _layer_norm(x, g, b):
    mu = jnp.mean(x, axis=-1, keepdims=True)
    xc = x - mu
    var = jnp.mean(xc * xc, axis=-1, keepdims=True)
    return xc * lax.rsqrt(var + LN_EPS) * g + b


def _sigmoid(x):
    return 1.0 / (1.0 + jnp.exp(-x))


def _silu(x):
    return x * _sigmoid(x)


def _softplus(x):
    return jnp.maximum(x, 0.0) + jnp.log1p(jnp.exp(-jnp.abs(x)))


def _log_sigmoid(x):
    return jnp.minimum(x, 0.0) - jnp.log1p(jnp.exp(-jnp.abs(x)))


def _gelu_tanh(x):
    c = math.sqrt(2.0 / math.pi)
    return x * (0.5 * (1.0 + jnp.tanh(c * (x + 0.044715 * (x * x * x)))))


def _mm(a, b):
    return jnp.dot(a.astype(BF16), b.astype(BF16), preferred_element_type=F32)


def _mm_nt(a, b):
    return lax.dot_general(a.astype(BF16), b.astype(BF16), (((1,), (1,)), ((), ())),
                           preferred_element_type=F32)


def _mm_tn(a, b):
    return lax.dot_general(a.astype(BF16), b.astype(BF16), (((0,), (0,)), ((), ())),
                           preferred_element_type=F32)


def _mm_f32(a, b):
    return jnp.dot(a, b, precision=HI, preferred_element_type=F32)


def _iota(shape, dim):
    return lax.broadcasted_iota(jnp.int32, shape, dim)


def _mixer_kernel(first_layer, tile,
                  h_ref, lng_ref, lnb_ref, w_in_ref, wgate_ref, bgate_ref, gla_ng_ref,
                  sgu_ng_ref, sgu_nb_ref, sgu_w_ref, sgu_bias_ref,
                  convw_ref, convb_ref, dtb_ref, alog_ref, dskip_ref, ssd_ng_ref,
                  w_out_ref, ln1g_ref, ln1b_ref,
                  tri_ref, tri64_ref, e512_ref, ecol_ref, avg64_ref,
                  out_ref,
                  hs_s, proj_s, convbuf_s, xbc_s, mix_s, gla_state_s, ssd_state_s):
    t = pl.program_id(1)

    @pl.when(t == 0)
    def _():
        convbuf_s[0:SUBLANES, :] = jnp.zeros((SUBLANES, SSD_CONV_DIM), F32)
        gla_state_s[...] = jnp.zeros_like(gla_state_s)
        ssd_state_s[...] = jnp.zeros_like(ssd_state_s)

    h = h_ref[...]
    if first_layer:
        h = _layer_norm(h, lng_ref[...], lnb_ref[...])
    hs_s[...] = h
    proj_s[...] = jnp.dot(h.astype(BF16), w_in_ref[...], preferred_element_type=F32)

    convbuf_s[SUBLANES:SUBLANES + tile, :] = proj_s[:, C_XBC:C_XBC + SSD_CONV_DIM]
    acc = convb_ref[...] + convw_ref[SSD_CONV - 1:SSD_CONV, :] * convbuf_s[SUBLANES:SUBLANES + tile, :]
    for k in range(SSD_CONV - 1):
        off = SUBLANES - (SSD_CONV - 1) + k
        acc = acc + convw_ref[k:k + 1, :] * convbuf_s[off:off + tile, :]
    xbc_s[...] = _silu(acc)
    convbuf_s[0:SUBLANES, :] = convbuf_s[tile:tile + SUBLANES, :]

    a_cont = -jnp.exp(alog_ref[...])

    def block(i, carry):
        r0 = pl.multiple_of(i * BLK, BLK)
        rows = pl.ds(r0, BLK)
        row_i = _iota((BLK, BLK), 0)
        col_i = _iota((BLK, BLK), 1)
        sm = proj_s[rows, C_SM:C_SM + LANES]

        q = proj_s[rows, C_Q:C_Q + GLA_KEY] * (GLA_DK ** -0.5)
        k = proj_s[rows, C_K:C_K + GLA_KEY]
        v = proj_s[rows, C_V:C_V + GLA_WIDTH]
        gate = _mm(sm, wgate_ref[...]) + bgate_ref[...]
        log_a = _log_sigmoid(gate) * (1.0 / GLA_TAU)
        cum = _mm_f32(tri64_ref[...], log_a)
        first_half = row_i < GLA_CHUNK
        c_mid = GLA_CHUNK // 2 - 1
        midb = jnp.where(first_half, cum[c_mid:c_mid + 1, :],
                         cum[GLA_CHUNK + c_mid:GLA_CHUNK + c_mid + 1, :])
        last0 = cum[GLA_CHUNK - 1:GLA_CHUNK, :]
        last1 = cum[BLK - 1:BLK, :]
        lastb = jnp.where(first_half, last0, last1)
        q_in = q * jnp.exp(cum)
        k_in = k * jnp.exp(lastb - cum)
        q_t = q * jnp.exp(cum - midb)
        k_t = k * jnp.exp(midb - cum)
        khead = col_i // GLA_DK
        qstack = jnp.concatenate(
            [jnp.where(khead == hh, q_t, 0.0) for hh in range(GLA_HEADS)], axis=0)
        p = _mm_nt(qstack, k_t)
        causal = (row_i >= col_i) & ((row_i // GLA_CHUNK) == (col_i // GLA_CHUNK))
        causal4 = jnp.concatenate([causal] * GLA_HEADS, axis=0)
        p = jnp.where(causal4, p, 0.0)
        o_full = _mm(p, v)
        vhead = _iota((BLK, GLA_WIDTH), 1) // GLA_DV
        o_gla = jnp.zeros((BLK, GLA_WIDTH), F32)
        for hh in range(GLA_HEADS):
            o_gla = o_gla + jnp.where(vhead == hh, o_full[hh * BLK:(hh + 1) * BLK, :], 0.0)
        st = gla_state_s[...]
        bd = (_iota((GLA_WIDTH, GLA_KEY), 0) // GLA_DV) == (_iota((GLA_WIDTH, GLA_KEY), 1) // GLA_DK)
        o_inter = []
        for c, last in ((0, last0), (1, last1)):
            rs = slice(c * GLA_CHUNK, (c + 1) * GLA_CHUNK)
            o_inter.append(_mm_nt(q_in[rs], st))
            upd = _mm_tn(v[rs], k_in[rs])
            st = jnp.where(bd, st * jnp.exp(last) + upd, 0.0)
        gla_state_s[...] = st
        o_gla = o_gla + jnp.concatenate(o_inter, axis=0)
        ms = _mm_f32(o_gla * o_gla, avg64_ref[...])
        g_gate = proj_s[rows, C_G:C_G + GLA_WIDTH]
        mix_s[rows, 0:GLA_WIDTH] = (o_gla * lax.rsqrt(ms + LN_EPS) * gla_ng_ref[...]) * _silu(g_gate)

        su = _gelu_tanh(proj_s[rows, C_SU:C_SU + SGU_WIDTH])
        sv = _layer_norm(_gelu_tanh(proj_s[rows, C_SV:C_SV + SGU_WIDTH]),
                         sgu_ng_ref[...], sgu_nb_ref[...])
        sgroup = _iota((BLK, SGU_WIDTH), 1) // SGU_DG
        tril = row_i >= col_i
        mixed = sgu_bias_ref[...]
        for gg in range(SGU_GROUPS):
            w_g = jnp.where(tril, sgu_w_ref[gg], 0.0)
            mixed = mixed + _mm(w_g, jnp.where(sgroup == gg, sv, 0.0))
        mix_s[rows, GLA_WIDTH:GLA_WIDTH + SGU_WIDTH] = su * mixed

        z = proj_s[rows, C_Z:C_Z + SSD_WIDTH]
        xs = xbc_s[rows, 0:SSD_WIDTH]
        dt = _softplus(sm + dtb_ref[...])
        a = dt * a_cont
        a_cs = _mm_f32(tri_ref[...], a)
        dt_e = _mm_f32(dt, e512_ref[...])
        acs_e = _mm_f32(a_cs, e512_ref[...])
        atot_e = acs_e[BLK - 1:BLK, :]
        colb = _mm_f32(a_cs, ecol_ref[...])
        xdt = xs * dt_e
        dskip = dskip_ref[...]
        ngrp = SSD_STATE
        hp = SSD_HEADS // SSD_GROUPS
        gw = SSD_WIDTH // SSD_GROUPS
        phead = _iota((BLK, gw), 1) // SSD_HEAD_DIM
        for gg in range(SSD_GROUPS):
            bm = xbc_s[rows, SSD_WIDTH + gg * ngrp:SSD_WIDTH + (gg + 1) * ngrp]
            cm = xbc_s[rows, SSD_WIDTH + (SSD_GROUPS + gg) * ngrp:SSD_WIDTH + (SSD_GROUPS + gg + 1) * ngrp]
            xdt_g = xdt[:, gg * gw:(gg + 1) * gw]
            acs_g = acs_e[:, gg * gw:(gg + 1) * gw]
            atot_g = atot_e[:, gg * gw:(gg + 1) * gw]
            cb = _mm_nt(cm, bm)
            y_g = jnp.zeros((BLK, gw), F32)
            for rr in range(hp):
                hh = gg * hp + rr
                col_h = colb[:, hh * LANES:(hh + 1) * LANES]
                lmat = jnp.where(tril, jnp.exp(col_h - col_h.T), 0.0)
                y_g = y_g + _mm(cb * lmat, jnp.where(phead == rr, xdt_g, 0.0))
            s_prev = ssd_state_s[gg]
            y_g = y_g + _mm(cm, s_prev) * jnp.exp(acs_g)
            ssd_state_s[gg] = s_prev * jnp.exp(atot_g) + _mm_tn(bm, xdt_g * jnp.exp(atot_g - acs_g))
            xs_g = xs[:, gg * gw:(gg + 1) * gw]
            y_g = y_g + xs_g * dskip[:, gg * gw:(gg + 1) * gw]
            y_g = y_g * _silu(z[:, gg * gw:(gg + 1) * gw])
            ms_g = jnp.mean(y_g * y_g, axis=-1, keepdims=True)
            c0 = GLA_WIDTH + SGU_WIDTH + gg * gw
            mix_s[rows, c0:c0 + gw] = y_g * lax.rsqrt(ms_g + LN_EPS) * ssd_ng_ref[:, gg * gw:(gg + 1) * gw]
        return carry

    lax.fori_loop(0, tile // BLK, block, 0)

    mix = jnp.dot(mix_s[...].astype(BF16), w_out_ref[...], preferred_element_type=F32)
    out_ref[...] = _layer_norm(DEEPNORM_ALPHA * hs_s[...] + mix, ln1g_ref[...], ln1b_ref[...])


def _ffn_kernel(tile, h_ref, wup_ref, cw_ref, cb_ref, wdn_ref, g_ref, b_ref, out_ref,
                buf_s, carry_s, act_s):
    t = pl.program_id(1)

    @pl.when(t == 0)
    def _():
        carry_s[...] = jnp.zeros_like(carry_s)

    h = h_ref[...]
    hb = h.astype(BF16)
    for c in range(FFN_NC):
        slot = c % 2
        u = jnp.dot(hb, wup_ref[c], preferred_element_type=F32)
        buf_s[slot, 0:SUBLANES, :] = carry_s[c]
        buf_s[slot, SUBLANES:SUBLANES + tile, :] = u
        cw = cw_ref[c]
        y = cb_ref[c] + cw[FFN_CONV - 1:FFN_CONV, :] * u
        for k in range(FFN_CONV - 1):
            off = SUBLANES - (FFN_CONV - 1) + k
            y = y + cw[k:k + 1, :] * buf_s[slot, off:off + tile, :]
        carry_s[c] = u[tile - SUBLANES:tile, :]
        act = _silu(y[:, 0:FFN_CW]) * y[:, FFN_CW:2 * FFN_CW]
        act_s[:, c * FFN_CW:(c + 1) * FFN_CW] = act.astype(BF16)
    down = jnp.dot(act_s[...], wdn_ref[...], preferred_element_type=F32)
    out_ref[...] = _layer_norm(DEEPNORM_ALPHA * h + down, g_ref[...], b_ref[...])


def _const_spec(shape):
    nd = len(shape)
    return pl.BlockSpec(shape, lambda b, t: (0,) * nd, pipeline_mode=pl.Buffered(1))


def _selection_constants():
    r = np.arange(BLK)
    tri = (r[:, None] >= r[None, :]).astype(np.float32)
    tri64 = tri * ((r[:, None] // GLA_CHUNK) == (r[None, :] // GLA_CHUNK))
    e512 = np.zeros((LANES, SSD_WIDTH), np.float32)
    ecol = np.zeros((LANES, SSD_HEADS * LANES), np.float32)
    for hh in range(SSD_HEADS):
        e512[SM_DT + hh, hh * SSD_HEAD_DIM:(hh + 1) * SSD_HEAD_DIM] = 1.0
        ecol[SM_DT + hh, hh * LANES:(hh + 1) * LANES] = 1.0
    c = np.arange(GLA_WIDTH)
    avg64 = ((c[:, None] // GLA_DV) == (c[None, :] // GLA_DV)).astype(np.float32) / GLA_DV
    return tri, tri64, e512, ecol, avg64


def _row(v):
    return v.reshape(1, -1).astype(F32)


def _pad_small(v, offset):
    return jnp.zeros((1, LANES), F32).at[0, offset:offset + v.shape[0]].set(v.astype(F32))


def _mixer_call(first_layer, h, batch, seq, tile, ln_in_g, ln_in_b, w_in, gla_w_gate, gla_b_gate,
                gla_norm_g, sgu_norm_g, sgu_norm_b, sgu_w, sgu_b, ssd_conv_w, ssd_conv_b,
                ssd_dt_bias, ssd_a_log, ssd_d, ssd_norm_g, w_out, ln1_g, ln1_b):
    nt = seq // tile
    o = np.cumsum([0, GLA_KEY, GLA_KEY, GLA_WIDTH, GLA_WIDTH, GLA_GATE_RANK, SGU_WIDTH, SGU_WIDTH,
                   SSD_WIDTH, SSD_CONV_DIM, SSD_HEADS])
    seg = lambda j: w_in[:, o[j]:o[j + 1]]
    small = jnp.zeros((D_MODEL, LANES), w_in.dtype)
    small = small.at[:, 0:GLA_GATE_RANK].set(seg(4)).at[:, SM_DT:SM_DT + SSD_HEADS].set(seg(9))
    w_in_p = jnp.concatenate([seg(0), seg(1), seg(2), seg(3), seg(5), seg(6), seg(7), seg(8), small],
                             axis=1).astype(BF16)
    wgate_p = jnp.zeros((LANES, GLA_KEY), F32).at[0:GLA_GATE_RANK, :].set(gla_w_gate).astype(BF16)
    sgu_bias = jnp.repeat(sgu_b.T, SGU_DG, axis=1).astype(F32)
    consts = [jnp.asarray(cst) for cst in _selection_constants()]
    args = [
        h, _row(ln_in_g), _row(ln_in_b), w_in_p, wgate_p, _row(gla_b_gate),
        _row(jnp.tile(gla_norm_g, GLA_HEADS)), _row(sgu_norm_g), _row(sgu_norm_b),
        sgu_w.astype(F32), sgu_bias, ssd_conv_w.astype(F32), _row(ssd_conv_b),
        _pad_small(ssd_dt_bias, SM_DT), _pad_small(ssd_a_log, SM_DT),
        _row(jnp.repeat(ssd_d, SSD_HEAD_DIM)), _row(ssd_norm_g),
        w_out.astype(BF16), _row(ln1_g), _row(ln1_b),
    ] + consts
    tok_spec = pl.BlockSpec((tile, D_MODEL), lambda b, t: (b * nt + t, 0))
    in_specs = [tok_spec] + [_const_spec(a.shape) for a in args[1:]]
    return pl.pallas_call(
        functools.partial(_mixer_kernel, first_layer, tile),
        grid=(batch, nt),
        in_specs=in_specs,
        out_specs=tok_spec,
        out_shape=jax.ShapeDtypeStruct((batch * seq, D_MODEL), F32),
        scratch_shapes=[
            pltpu.VMEM((tile, D_MODEL), F32),
            pltpu.VMEM((tile, N_PROJ), F32),
            pltpu.VMEM((tile + SUBLANES, SSD_CONV_DIM), F32),
            pltpu.VMEM((tile, SSD_CONV_DIM), F32),
            pltpu.VMEM((tile, D_MODEL), F32),
            pltpu.VMEM((GLA_WIDTH, GLA_KEY), F32),
            pltpu.VMEM((SSD_GROUPS, SSD_STATE, SSD_WIDTH // SSD_GROUPS), F32),
        ],
        compiler_params=pltpu.CompilerParams(
            dimension_semantics=("arbitrary", "arbitrary"),
            vmem_limit_bytes=VMEM_LIMIT_BYTES),
        name="mixer_first" if first_layer else "mixer",
    )(*args)


def _ffn_call(h, batch, seq, tile, w_up, conv_w, conv_b, w_down, ln_g, ln_b):
    nt = seq // tile

    def chunked(m):
        lead = m.shape[:-1]
        gv = m.reshape(lead + (2, FFN_NC, FFN_CW))
        gv = jnp.moveaxis(gv, -2, 0)
        return gv.reshape((FFN_NC,) + lead + (2 * FFN_CW,))

    wup_c = chunked(w_up).astype(BF16)
    cw_c = chunked(conv_w).astype(F32)
    cb_c = chunked(conv_b.reshape(1, -1)).astype(F32)
    args = [h, wup_c, cw_c, cb_c, w_down.astype(BF16), _row(ln_g), _row(ln_b)]
    tok_spec = pl.BlockSpec((tile, D_MODEL), lambda b, t: (b * nt + t, 0))
    in_specs = [tok_spec] + [_const_spec(a.shape) for a in args[1:]]
    return pl.pallas_call(
        functools.partial(_ffn_kernel, tile),
        grid=(batch, nt),
        in_specs=in_specs,
        out_specs=tok_spec,
        out_shape=jax.ShapeDtypeStruct((batch * seq, D_MODEL), F32),
        scratch_shapes=[
            pltpu.VMEM((2, tile + SUBLANES, 2 * FFN_CW), F32),
            pltpu.VMEM((FFN_NC, SUBLANES, 2 * FFN_CW), F32),
            pltpu.VMEM((tile, D_FF), BF16),
        ],
        compiler_params=pltpu.CompilerParams(
            dimension_semantics=("arbitrary", "arbitrary"),
            vmem_limit_bytes=VMEM_LIMIT_BYTES),
        name="conv_ffn",
    )(*args)


def kernel(x, ln_in_g, ln_in_b, w_in, gla_w_gate, gla_b_gate, gla_norm_g, sgu_norm_g, sgu_norm_b, sgu_w, sgu_b, ssd_conv_w, ssd_conv_b, ssd_dt_bias, ssd_a_log, ssd_d, ssd_norm_g, w_out, ln1_g, ln1_b, ffn_w_up, ffn_conv_w, ffn_conv_b, ffn_w_down, ln2_g, ln2_b):
    batch, seq, d = x.shape
    assert d == D_MODEL and seq % BLK == 0
    tile = 512 if seq % 512 == 0 else BLK
    h = x.reshape(batch * seq, d)
    for i in range(DEPTH):
        h = _mixer_call(i == 0, h, batch, seq, tile, ln_in_g, ln_in_b, w_in[i], gla_w_gate[i],
                        gla_b_gate[i], gla_norm_g[i], sgu_norm_g[i], sgu_norm_b[i], sgu_w[i],
                        sgu_b[i], ssd_conv_w[i], ssd_conv_b[i], ssd_dt_bias[i], ssd_a_log[i],
                        ssd_d[i], ssd_norm_g[i], w_out[i], ln1_g[i], ln1_b[i])
        h = _ffn_call(h, batch, seq, tile, ffn_w_up[i], ffn_conv_w[i], ffn_conv_b[i],
                      ffn_w_down[i], ln2_g[i], ln2_b[i])
    return h.reshape(batch, seq, d)
```
